```python
import jax, jax.numpy as jnp
from jax import lax
import numpy as np

D_MODEL = 1024
BATCH = 1
SEQ = 16384
DEPTH = 4
DEC_BATCH = 32
DEC_SEQ = 2048
PAST_LEN = 128

N_Q_HEADS = 8
N_KV_HEADS = 2
HEAD_DIM = 64
Q_DIM = N_Q_HEADS * HEAD_DIM
KV_DIM = N_KV_HEADS * HEAD_DIM
GQA_GROUP = N_Q_HEADS // N_KV_HEADS
ROT_DIM = HEAD_DIM // 4
ROPE_THETA = 500000.0
WINDOW = 128
BLOCK = 128
LRU_WIDTH = 512
LRU_HEADS = 8
LRU_BLOCK = LRU_WIDTH // LRU_HEADS
LRU_C = 8.0
CONV_WIDTH = 4
CONV_LEFT = 2
MIX_WIDTH = Q_DIM + LRU_WIDTH
IN_DIM = Q_DIM + 2 * KV_DIM + 2 * LRU_WIDTH
SPLITS = [Q_DIM, Q_DIM + KV_DIM, Q_DIM + 2 * KV_DIM, Q_DIM + 2 * KV_DIM + LRU_WIDTH]
N_EXPERTS = 16
EXPERT_FF = 2 * D_MODEL
CAPACITY_FACTOR = 2
EPS = 1e-6

kernel_name = 'hymba_swa_rglru_ec_moe_encoder'


def rms_norm(x, g):
    xf = x.astype(jnp.float32)
    y = xf * lax.rsqrt(jnp.mean(xf * xf, axis=-1, keepdims=True) + EPS)
    return (y * g.astype(jnp.float32)).astype(x.dtype)


def partial_rope(x, pos):
    half = ROT_DIM // 2
    inv_freq = ROPE_THETA ** (-jnp.arange(half, dtype=jnp.float32) * 2.0 / ROT_DIM)
    ang = pos.astype(jnp.float32)[:, None] * inv_freq[None, :]
    cos = jnp.cos(ang)[None, :, None, :]
    sin = jnp.sin(ang)[None, :, None, :]
    xr = x[..., :ROT_DIM].astype(jnp.float32)
    x1, x2 = xr[..., :half], xr[..., half:]
    rot = jnp.concatenate([x1 * cos - x2 * sin, x2 * cos + x1 * sin], axis=-1).astype(x.dtype)
    return jnp.concatenate([rot, x[..., ROT_DIM:]], axis=-1)


def band_blocks(t):
    B, S = t.shape[0], t.shape[1]
    nb = S // BLOCK
    tp = jnp.pad(t, ((0, 0), (BLOCK, BLOCK), (0, 0), (0, 0)))
    tb = tp.reshape(B, nb + 2, BLOCK, t.shape[2], t.shape[3])
    return jnp.concatenate([tb[:, :-2], tb[:, 1:-1], tb[:, 2:]], axis=2)


def window_attention(q, k, v, sink):
    B, S = q.shape[0], q.shape[1]
    nb = S // BLOCK
    qb = q.reshape(B, nb, BLOCK, N_KV_HEADS, GQA_GROUP, HEAD_DIM)
    kb = band_blocks(k)
    vb = band_blocks(v)
    s = jnp.einsum('bnqhgd,bnkhd->bnhgqk', qb, kb,
                   preferred_element_type=jnp.float32) * (HEAD_DIM ** -0.5)
    qpos = jnp.arange(nb)[:, None, None] * BLOCK + jnp.arange(BLOCK)[None, :, None]
    kpos = jnp.arange(nb)[:, None, None] * BLOCK - BLOCK + jnp.arange(3 * BLOCK)[None, None, :]
    valid = (jnp.abs(qpos - kpos) <= WINDOW) & (kpos >= 0) & (kpos < S)
    s = jnp.where(valid[None, :, None, None], s, -jnp.inf)
    sk = sink.astype(jnp.float32).reshape(N_KV_HEADS, GQA_GROUP)[None, None, :, :, None, None]
    m = jnp.maximum(jnp.max(s, axis=-1, keepdims=True), sk)
    p = jnp.exp(s - m)
    denom = jnp.sum(p, axis=-1, keepdims=True) + jnp.exp(sk - m)
    p = (p / denom).astype(v.dtype)
    o = jnp.einsum('bnhgqk,bnkhd->bnqhgd', p, vb)
    return o.reshape(B, S, Q_DIM)


def _linear_combine(left, right):
    a_l, b_l = left
    a_r, b_r = right
    return a_l * a_r, a_r * b_l + b_r


def rg_lru(x, w_a, b_a, w_x, b_x, lam, reverse):
    B, S, _ = x.shape
    xh = x.reshape(B, S, LRU_HEADS, LRU_BLOCK)
    gate_a = jnp.einsum('bshi,hij->bshj', xh, w_a).reshape(B, S, LRU_WIDTH) + b_a
    gate_x = jnp.einsum('bshi,hij->bshj', xh, w_x).reshape(B, S, LRU_WIDTH) + b_x
    r = jax.nn.sigmoid(gate_a.astype(jnp.float32))
    i = jax.nn.sigmoid(gate_x.astype(jnp.float32))
    log_a = -LRU_C * r * jax.nn.softplus(-lam.astype(jnp.float32))
    a = jnp.exp(log_a)
    b = jnp.sqrt(-jnp.expm1(2.0 * log_a)) * (i * x.astype(jnp.float32))
    _, h = lax.associative_scan(_linear_combine, (a, b), reverse=reverse, axis=1)
    return h


def recurrent_group(xr, gate, conv_w, conv_b, w_a, b_a, w_x, b_x, lam):
    xc = lax.conv_general_dilated(
        xr, conv_w[:, None, :], window_strides=(1,),
        padding=[(CONV_LEFT, CONV_WIDTH - 1 - CONV_LEFT)],
        dimension_numbers=('NWC', 'WIO', 'NWC'),
        feature_group_count=LRU_WIDTH) + conv_b
    h = (rg_lru(xc, w_a[0], b_a[0], w_x[0], b_x[0], lam[0], False)
         + rg_lru(xc, w_a[1], b_a[1], w_x[1], b_x[1], lam[1], True))
    return h.astype(xr.dtype) * jax.nn.gelu(gate)


def mixer_block(x, g_norm, w_in, sink, g_attn, g_lru, conv_w, conv_b,
                w_a, b_a, w_x, b_x, lam, w_out):
    B, S, _ = x.shape
    h = rms_norm(x, g_norm)
    z = h @ w_in
    q, k, v, xr, gate = jnp.split(z, SPLITS, axis=-1)
    pos = jnp.arange(S)
    q = partial_rope(q.reshape(B, S, N_Q_HEADS, HEAD_DIM), pos)
    k = partial_rope(k.reshape(B, S, N_KV_HEADS, HEAD_DIM), pos)
    v = v.reshape(B, S, N_KV_HEADS, HEAD_DIM)
    attn = window_attention(q, k, v, sink)
    rec = recurrent_group(xr, gate, conv_w, conv_b, w_a, b_a, w_x, b_x, lam)
    mixed = jnp.concatenate([rms_norm(attn, g_attn), rms_norm(rec, g_lru)], axis=-1)
    return mixed @ w_out


def expert_choice_ffn(x, w_router, w_gate, w_up, w_down):
    B, S, D = x.shape
    n = B * S
    cap = max(1, CAPACITY_FACTOR * n // N_EXPERTS)
    xf = x.reshape(n, D)
    probs = jax.nn.softmax((xf @ w_router).astype(jnp.float32), axis=-1)
    gates, idx = lax.top_k(probs.T, cap)

    def one_expert(args):
        wg, wu, wd, ix, gg = args
        xs = xf[ix]
        hdn = jax.nn.silu(xs @ wg) * (xs @ wu)
        return (hdn @ wd) * gg[:, None].astype(x.dtype)

    out = lax.map(one_expert, (w_gate, w_up, w_down, idx, gates))
    y = jnp.zeros((n, D), x.dtype).at[idx.reshape(-1)].add(out.reshape(-1, D))
    return y.reshape(B, S, D)


def trunk(x, norm1, w_in, attn_sink, g_attn_out, g_lru_out, conv_w, conv_b,
          lru_w_a, lru_b_a, lru_w_x, lru_b_x, lru_lambda, w_out, norm2,
          w_router, w_gate, w_up, w_down, final_norm):
    for l in range(DEPTH):
        x = x + mixer_block(x, norm1[l], w_in[l], attn_sink[l], g_attn_out[l], g_lru_out[l],
                            conv_w[l], conv_b[l], lru_w_a[l], lru_b_a[l], lru_w_x[l],
                            lru_b_x[l], lru_lambda[l], w_out[l])
        x = x + expert_choice_ffn(rms_norm(x, norm2[l]), w_router[l], w_gate[l],
                                  w_up[l], w_down[l])
    return rms_norm(x, final_norm)


def setup_inputs(seed: int = 0) -> dict:
    key = jax.random.key(seed)
    ks = jax.random.split(key, 24)
    f32 = jnp.float32
    L = DEPTH

    def nrm(k, shape, scale):
        return jax.random.normal(k, shape, f32) * scale

    u = jax.random.uniform(ks[13], (L, 2, LRU_WIDTH), f32, 0.9, 0.999)
    a0 = u ** (1.0 / LRU_C)
    lru_lambda = jnp.log(a0) - jnp.log1p(-a0)
    return {
        'x_prompt': nrm(ks[0], (BATCH, SEQ, D_MODEL), 1.0),
        'x_sample': nrm(ks[1], (DEC_BATCH, DEC_SEQ, D_MODEL), 1.0),
        'norm1': 1.0 + nrm(ks[2], (L, D_MODEL), 0.02),
        'w_in': nrm(ks[3], (L, D_MODEL, IN_DIM), D_MODEL ** -0.5),
        'attn_sink': nrm(ks[4], (L, N_Q_HEADS), 0.5),
        'g_attn_out': 1.0 + nrm(ks[5], (L, Q_DIM), 0.02),
        'g_lru_out': 1.0 + nrm(ks[6], (L, LRU_WIDTH), 0.02),
        'conv_w': nrm(ks[7], (L, CONV_WIDTH, LRU_WIDTH), CONV_WIDTH ** -0.5),
        'conv_b': nrm(ks[8], (L, LRU_WIDTH), 0.01),
        'lru_w_a': nrm(ks[9], (L, 2, LRU_HEADS, LRU_BLOCK, LRU_BLOCK), LRU_BLOCK ** -0.5),
        'lru_b_a': nrm(ks[10], (L, 2, LRU_WIDTH), 0.01),
        'lru_w_x': nrm(ks[11], (L, 2, LRU_HEADS, LRU_BLOCK, LRU_BLOCK), LRU_BLOCK ** -0.5),
        'lru_b_x': nrm(ks[12], (L, 2, LRU_WIDTH), 0.01),
        'lru_lambda': lru_lambda,
        'w_out': nrm(ks[14], (L, MIX_WIDTH, D_MODEL), MIX_WIDTH ** -0.5),
        'norm2': 1.0 + nrm(ks[15], (L, D_MODEL), 0.02),
        'w_router': nrm(ks[16], (L, D_MODEL, N_EXPERTS), D_MODEL ** -0.5),
        'w_gate': nrm(ks[17], (L, N_EXPERTS, D_MODEL, EXPERT_FF), D_MODEL ** -0.5),
        'w_up': nrm(ks[18], (L, N_EXPERTS, D_MODEL, EXPERT_FF), D_MODEL ** -0.5),
        'w_down': nrm(ks[19], (L, N_EXPERTS, EXPERT_FF, D_MODEL), EXPERT_FF ** -0.5),
        'final_norm': 1.0 + nrm(ks[20], (D_MODEL,), 0.02),
    }


def reference(x_prompt, x_sample, norm1, w_in, attn_sink, g_attn_out, g_lru_out, conv_w,
              conv_b, lru_w_a, lru_b_a, lru_w_x, lru_b_x, lru_lambda, w_out, norm2,
              w_router, w_gate, w_up, w_down, final_norm):
    weights = (norm1, w_in, attn_sink, g_attn_out, g_lru_out, conv_w, conv_b,
               lru_w_a, lru_b_a, lru_w_x, lru_b_x, lru_lambda, w_out, norm2,
               w_router, w_gate, w_up, w_down, final_norm)
    y_prompt = trunk(x_prompt, *weights)
    y_sample = trunk(x_sample, *weights)
    return (y_prompt, y_sample)
```

```python
import functools

import jax
import jax.numpy as jnp
from jax import lax
from jax.experimental import pallas as pl
from jax.experimental.pallas import tpu as pltpu

F32 = jnp.float32
BF16 = jnp.bfloat16
I32 = jnp.int32

D_MODEL = 1024
N_Q_HEADS = 8
N_KV_HEADS = 2
HEAD_DIM = 64
Q_DIM = N_Q_HEADS * HEAD_DIM
KV_DIM = N_KV_HEADS * HEAD_DIM
ROT_DIM = HEAD_DIM // 4
ROPE_THETA = 500000.0
WINDOW = 128
LRU_WIDTH = 512
LRU_HEADS = 8
LRU_BLOCK = LRU_WIDTH // LRU_HEADS
LRU_C = 8.0
IN_DIM = Q_DIM + 2 * KV_DIM + 2 * LRU_WIDTH
N_EXPERTS = 16
EXPERT_FF = 2 * D_MODEL
CAPACITY_FACTOR = 2
EPS = 1e-6

LANES = 128
SUBLANES = 8
ROW_CHUNKS = D_MODEL // LANES

TM_IN = 512
TQ = 512
TM_REC = 512
TM_OUT = 512
TILE = LANES
DSP_TILES = 4
HALF = 128
FFN_ROWS = 1024
FFN_FF = 512
CMB_T = 256
CMB_SLAB = 80
VMEM_LIMIT = 56 * 1024 * 1024


def _cparams(sem):
    return pltpu.CompilerParams(dimension_semantics=sem, vmem_limit_bytes=VMEM_LIMIT)


def _seq_bounds(t0, n0, s0, s1):
    in0 = t0 < n0
    start = jnp.where(in0, (t0 // s0) * s0, n0 + ((t0 - n0) // s1) * s1)
    return start, start + jnp.where(in0, s0, s1)


def _mix_in_kernel(x_ref, g_ref, w_ref, c_ref, s1_ref, s2_ref, q_ref, k_ref, v_ref, xr_ref, gt_ref):
    x = x_ref[...]
    h = (x * lax.rsqrt(jnp.mean(x * x, axis=-1, keepdims=True) + EPS)) * g_ref[...]
    z = jnp.dot(h.astype(BF16), w_ref[...], preferred_element_type=F32)
    q = z[:, :Q_DIM]
    k = z[:, Q_DIM:Q_DIM + KV_DIM]
    c = c_ref[...]
    s1 = s1_ref[...]
    s2 = s2_ref[...]
    kr = k * c + pltpu.roll(k, KV_DIM - ROT_DIM // 2, 1) * s1 + pltpu.roll(k, ROT_DIM // 2, 1) * s2
    rep = Q_DIM // LANES
    c4 = jnp.concatenate([c] * rep, axis=1)
    s14 = jnp.concatenate([s1] * rep, axis=1)
    s24 = jnp.concatenate([s2] * rep, axis=1)
    qr = q * c4 + pltpu.roll(q, Q_DIM - ROT_DIM // 2, 1) * s14 + pltpu.roll(q, ROT_DIM // 2, 1) * s24
    qs = qr * (HEAD_DIM ** -0.5)
    low = lax.broadcasted_iota(I32, (qs.shape[0], LANES), 1) < HEAD_DIM
    group = N_Q_HEADS // N_KV_HEADS
    for hd in range(N_Q_HEADS):
        m, g = hd // 2, hd // group
        slab = qs[:, m * LANES:(m + 1) * LANES]
        src = slab if (hd % 2) == g else pltpu.roll(slab, HEAD_DIM, 1)
        keep = low if g == 0 else jnp.logical_not(low)
        q_ref[:, hd * LANES:(hd + 1) * LANES] = jnp.where(keep, src, 0.0).astype(BF16)
    k_ref[...] = kr.astype(BF16)
    v_ref[...] = z[:, Q_DIM + KV_DIM:Q_DIM + 2 * KV_DIM].astype(BF16)
    xr_ref[...] = z[:, Q_DIM + 2 * KV_DIM:Q_DIM + 2 * KV_DIM + LRU_WIDTH]
    gt_ref[...] = z[:, Q_DIM + 2 * KV_DIM + LRU_WIDTH:]


def _mix_in(x, g, w_bf, rope, n0, s0, s1):
    n = x.shape[0]
    tm = TM_IN

    def pos_map(i):
        t0 = i * tm
        return (jnp.where(t0 < n0, lax.rem(t0, s0), lax.rem(t0 - n0, s1)) // tm, 0)

    row = lambda i: (i, 0)
    const = lambda i: (0, 0)
    tab = pl.BlockSpec((tm, LANES), pos_map)
    return pl.pallas_call(
        _mix_in_kernel,
        grid=(n // tm,),
        in_specs=[pl.BlockSpec((tm, D_MODEL), row), pl.BlockSpec((1, D_MODEL), const),
                  pl.BlockSpec((D_MODEL, IN_DIM), const), tab, tab, tab],
        out_specs=[pl.BlockSpec((tm, N_Q_HEADS * LANES), row), pl.BlockSpec((tm, KV_DIM), row),
                   pl.BlockSpec((tm, KV_DIM), row), pl.BlockSpec((tm, LRU_WIDTH), row),
                   pl.BlockSpec((tm, LRU_WIDTH), row)],
        out_shape=[jax.ShapeDtypeStruct((n, N_Q_HEADS * LANES), BF16), jax.ShapeDtypeStruct((n, KV_DIM), BF16),
                   jax.ShapeDtypeStruct((n, KV_DIM), BF16), jax.ShapeDtypeStruct((n, LRU_WIDTH), F32),
                   jax.ShapeDtypeStruct((n, LRU_WIDTH), F32)],
        compiler_params=_cparams(("parallel",)),
        name="mix_in",
    )(x, g, w_bf, *rope)


def _attention_kernel(sink_ref, q_ref, kp_ref, kc_ref, kn_ref, vp_ref, vc_ref, vn_ref, o_ref, *, n0, s0, s1):
    j = pl.program_id(0)
    t0 = j * TQ
    seq_lo, seq_hi = _seq_bounds(t0, n0, s0, s1)
    kcat = jnp.concatenate([kp_ref[...], kc_ref[...], kn_ref[...]], axis=0)
    vcat = jnp.concatenate([vp_ref[...], vc_ref[...], vn_ref[...]], axis=0)
    nkey = 3 * WINDOW
    group = N_Q_HEADS // N_KV_HEADS
    low = lax.broadcasted_iota(I32, (WINDOW, LANES), 1) < HEAD_DIM
    for b in range(TQ // WINDOW):
        kw = kcat[b * WINDOW:b * WINDOW + nkey]
        vw = vcat[b * WINDOW:b * WINDOW + nkey]
        qpos = t0 + b * WINDOW + lax.broadcasted_iota(I32, (WINDOW, nkey), 0)
        kpos = t0 + (b - 1) * WINDOW + lax.broadcasted_iota(I32, (WINDOW, nkey), 1)
        valid = (jnp.abs(qpos - kpos) <= WINDOW) & (kpos >= seq_lo) & (kpos < seq_hi)
        valid4 = jnp.concatenate([valid] * group, axis=0)
        outs = []
        for g in range(N_KV_HEADS):
            qs = jnp.concatenate([q_ref[b * WINDOW:(b + 1) * WINDOW, hd * LANES:(hd + 1) * LANES]
                                  for hd in range(group * g, group * (g + 1))], axis=0)
            s = lax.dot_general(qs, kw, (((1,), (1,)), ((), ())), preferred_element_type=F32)
            s = jnp.where(valid4, s, -jnp.inf)
            sink = jnp.concatenate(
                [jnp.full((WINDOW, 1), sink_ref[group * g + i], F32) for i in range(group)], axis=0)
            mx = jnp.maximum(jnp.max(s, axis=-1, keepdims=True), sink)
            p = jnp.exp(s - mx)
            denom = jnp.sum(p, axis=-1, keepdims=True) + jnp.exp(sink - mx)
            o = jnp.dot(p.astype(BF16), vw, preferred_element_type=F32) / denom
            outs.append(o)
        for m in range(Q_DIM // LANES):
            g = (2 * m) // group
            i0 = (2 * m) % group
            oe = outs[g][i0 * WINDOW:(i0 + 1) * WINDOW]
            oo = outs[g][(i0 + 1) * WINDOW:(i0 + 2) * WINDOW]
            if g == 0:
                res = jnp.where(low, oe, pltpu.roll(oo, HEAD_DIM, 1))
            else:
                res = jnp.where(low, pltpu.roll(oe, HEAD_DIM, 1), oo)
            o_ref[b * WINDOW:(b + 1) * WINDOW, m * LANES:(m + 1) * LANES] = res


def _attention(q, k, v, sink, n0, s0, s1):
    n = q.shape[0]
    r = TQ // WINDOW
    nblk = n // WINDOW
    prev = lambda j: (jnp.maximum(j * r - 1, 0), 0)
    cur = lambda j: (j, 0)
    nxt = lambda j: (jnp.minimum(j * r + r, nblk - 1), 0)
    small = lambda f: pl.BlockSpec((WINDOW, KV_DIM), f)
    big = pl.BlockSpec((TQ, KV_DIM), cur)
    return pl.pallas_call(
        functools.partial(_attention_kernel, n0=n0, s0=s0, s1=s1),
        grid=(n // TQ,),
        in_specs=[pl.BlockSpec(memory_space=pltpu.SMEM), pl.BlockSpec((TQ, N_Q_HEADS * LANES), cur),
                  small(prev), big, small(nxt), small(prev), big, small(nxt)],
        out_specs=pl.BlockSpec((TQ, Q_DIM), cur),
        out_shape=jax.ShapeDtypeStruct((n, Q_DIM), F32),
        compiler_params=_cparams(("parallel",)),
        name="attention",
    )(sink, q, k, k, k, v, v, v)


def _recurrent_kernel(xr_ref, hp_ref, hn_ref, cw_ref, cb_ref, w_ref, ba_ref, bx_ref, lam_ref, o_ref, carry_ref,
                      *, n0, s0, s1, ntiles):
    d = pl.program_id(0)
    i = pl.program_id(1)
    tm = TM_REC
    tile = jnp.where(d == 0, i, ntiles - 1 - i)
    t0 = tile * tm
    seq_lo, seq_hi = _seq_bounds(t0, n0, s0, s1)
    first = t0 == seq_lo
    last = t0 + tm == seq_hi

    @pl.when((d == 0) & (i == 0))
    def _():
        carry_ref[...] = jnp.zeros_like(carry_ref)

    xr = xr_ref[...]
    hp = jnp.where(first, 0.0, hp_ref[...])
    hn = jnp.where(last, 0.0, hn_ref[...])
    big = jnp.concatenate([hp, xr, hn], axis=0)
    cw = cw_ref[...]
    xc = cb_ref[...] + sum(big[SUBLANES - 2 + jj:SUBLANES - 2 + jj + tm] * cw[jj:jj + 1] for jj in range(4))
    xcb = xc.astype(BF16)
    pair = 2 * LRU_BLOCK
    ga, gx = [], []
    for p in range(LRU_WIDTH // pair):
        r = jnp.dot(xcb[:, p * pair:(p + 1) * pair], w_ref[0, p], preferred_element_type=F32)
        ga.append(r[:, :pair])
        gx.append(r[:, pair:])
    gate_a = jnp.concatenate(ga, axis=1) + ba_ref[0]
    gate_x = jnp.concatenate(gx, axis=1) + bx_ref[0]
    lam = lam_ref[0]
    softplus_neg_lam = jnp.maximum(-lam, 0.0) + jnp.log1p(jnp.exp(-jnp.abs(lam)))
    log_a = (-LRU_C * jax.nn.sigmoid(gate_a)) * softplus_neg_lam
    a = jnp.exp(log_a)
    b = jnp.sqrt(-jnp.tanh(log_a) * (a * a + 1.0)) * (jax.nn.sigmoid(gate_x) * xc)
    rows = lax.broadcasted_iota(I32, (tm, LRU_WIDTH), 0)

    def scan(a, b, fwd):
        sh = 1
        while sh < tm:
            if fwd:
                keep = rows >= sh
                a_s = pltpu.roll(a, sh, 0)
                b_s = pltpu.roll(b, sh, 0)
            else:
                keep = rows < tm - sh
                a_s = pltpu.roll(a, tm - sh, 0)
                b_s = pltpu.roll(b, tm - sh, 0)
            b = jnp.where(keep, a * b_s + b, b)
            a = jnp.where(keep, a * a_s, a)
            sh *= 2
        return a, b

    def run(fwd):
        boundary = first if fwd else last
        cin = jnp.where(boundary, 0.0, carry_ref[...])
        ca, cb = scan(a, b, fwd)
        h = ca * cin + cb
        o_ref[0] = h
        carry_ref[...] = h[tm - 1:tm] if fwd else h[0:1]

    @pl.when(d == 0)
    def _():
        run(True)

    @pl.when(d == 1)
    def _():
        run(False)


def _recurrent(xr, conv_w, conv_b, w_pairs, b_a, b_x, lam, n0, s0, s1):
    n = xr.shape[0]
    tm = TM_REC
    nt = n // tm
    r = tm // SUBLANES
    nb8 = n // SUBLANES
    tile_of = lambda d, i: jnp.where(d == 0, i, nt - 1 - i)
    cur = lambda d, i: (tile_of(d, i), 0)
    prev = lambda d, i: (jnp.maximum(tile_of(d, i) * r - 1, 0), 0)
    nxt = lambda d, i: (jnp.minimum(tile_of(d, i) * r + r, nb8 - 1), 0)
    const = lambda d, i: (0, 0)
    per_dir = lambda d, i: (d, 0, 0)
    return pl.pallas_call(
        functools.partial(_recurrent_kernel, n0=n0, s0=s0, s1=s1, ntiles=nt),
        grid=(2, nt),
        in_specs=[pl.BlockSpec((tm, LRU_WIDTH), cur), pl.BlockSpec((SUBLANES, LRU_WIDTH), prev),
                  pl.BlockSpec((SUBLANES, LRU_WIDTH), nxt), pl.BlockSpec((4, LRU_WIDTH), const),
                  pl.BlockSpec((1, LRU_WIDTH), const),
                  pl.BlockSpec((1, LRU_WIDTH // (2 * LRU_BLOCK), 2 * LRU_BLOCK, 4 * LRU_BLOCK),
                               lambda d, i: (d, 0, 0, 0)),
                  pl.BlockSpec((1, 1, LRU_WIDTH), per_dir), pl.BlockSpec((1, 1, LRU_WIDTH), per_dir),
                  pl.BlockSpec((1, 1, LRU_WIDTH), per_dir)],
        out_specs=pl.BlockSpec((1, tm, LRU_WIDTH), lambda d, i: (d, tile_of(d, i), 0)),
        out_shape=jax.ShapeDtypeStruct((2, n, LRU_WIDTH), F32),
        scratch_shapes=[pltpu.VMEM((1, LRU_WIDTH), F32)],
        compiler_params=_cparams(("arbitrary", "arbitrary")),
        name="recurrent",
    )(xr, xr, xr, conv_w, conv_b, w_pairs, b_a, b_x, lam)


def _rms(x, g):
    return (x * lax.rsqrt(jnp.mean(x * x, axis=-1, keepdims=True) + EPS)) * g


def _mix_out_kernel(x_ref, at_ref, h_ref, gt_ref, ga_ref, gl_ref, wo_ref, g2_ref, wr_ref, x1_ref, h2_ref, p_ref):
    gt = gt_ref[...]
    gelu = 0.5 * gt * (1.0 + jnp.tanh(0.7978845608028654 * (gt + 0.044715 * (gt * gt * gt))))
    rec = (h_ref[0] + h_ref[1]) * gelu
    mixed = jnp.concatenate([_rms(at_ref[...], ga_ref[...]), _rms(rec, gl_ref[...])], axis=1)
    x1 = x_ref[...] + jnp.dot(mixed.astype(BF16), wo_ref[...], preferred_element_type=F32)
    x1_ref[...] = x1
    h2 = _rms(x1, g2_ref[...])
    tm = h2.shape[0]
    for s in range(ROW_CHUNKS):
        h2_ref[pl.ds(s, tm, stride=ROW_CHUNKS), :] = h2[:, s * LANES:(s + 1) * LANES]
    logits = lax.dot_general(wr_ref[...], h2, (((1,), (1,)), ((), ())), precision=lax.Precision.HIGHEST,
                             preferred_element_type=F32)
    e = jnp.exp(logits - jnp.max(logits, axis=0, keepdims=True))
    p_ref[...] = e / jnp.sum(e, axis=0, keepdims=True)


def _mix_out(x, attn, h, gt, g_attn, g_lru, wo_bf, g2, wr_t):
    n = x.shape[0]
    tm = TM_OUT
    row = lambda i: (i, 0)
    const = lambda i: (0, 0)
    return pl.pallas_call(
        _mix_out_kernel,
        grid=(n // tm,),
        in_specs=[pl.BlockSpec((tm, D_MODEL), row), pl.BlockSpec((tm, Q_DIM), row),
                  pl.BlockSpec((2, tm, LRU_WIDTH), lambda i: (0, i, 0)), pl.BlockSpec((tm, LRU_WIDTH), row),
                  pl.BlockSpec((1, Q_DIM), const), pl.BlockSpec((1, LRU_WIDTH), const),
                  pl.BlockSpec((Q_DIM + LRU_WIDTH, D_MODEL), const), pl.BlockSpec((1, D_MODEL), const),
                  pl.BlockSpec((N_EXPERTS, D_MODEL), const)],
        out_specs=[pl.BlockSpec((tm, D_MODEL), row), pl.BlockSpec((tm * ROW_CHUNKS, LANES), row),
                   pl.BlockSpec((N_EXPERTS, tm), lambda i: (0, i))],
        out_shape=[jax.ShapeDtypeStruct((n, D_MODEL), F32), jax.ShapeDtypeStruct((n * ROW_CHUNKS, LANES), F32),
                   jax.ShapeDtypeStruct((N_EXPERTS, n), F32)],
        compiler_params=_cparams(("parallel",)),
        name="mix_out",
    )(x, attn, h, gt, g_attn, g_lru, wo_bf, g2, wr_t)


def _route_kernel(p_ref, gate_ref, row_ref, lidx_ref, cnt_ref, *, cap, row_base):
    ne, nr, _ = p_ref.shape
    p = p_ref[...]
    rows = ne * nr
    ri = lax.broadcasted_iota(I32, (LANES, LANES), 0)
    ci = lax.broadcasted_iota(I32, (LANES, LANES), 1)
    upper = (ri < ci).astype(BF16)
    ones = jnp.ones((LANES, LANES), BF16)
    rr = lax.broadcasted_iota(I32, (nr, nr), 0)
    rc = lax.broadcasted_iota(I32, (nr, nr), 1)
    lower = (rc < rr).astype(BF16)

    def count(mask):
        c = jnp.sum(mask.astype(F32), axis=1, keepdims=True)
        return jnp.sum(c, axis=2, keepdims=True)

    def prefix(mask):
        m2 = mask.astype(BF16).reshape(rows, LANES)
        within = jnp.dot(m2, upper, preferred_element_type=F32).reshape(ne, nr, LANES)
        tot = jnp.dot(m2, ones, preferred_element_type=F32).reshape(ne, nr, LANES)
        offs = jnp.stack([jnp.dot(lower, tot[e].astype(BF16), preferred_element_type=F32) for e in range(ne)])
        return within, offs, tot

    def step(i, prefix_bits):
        cand_bits = prefix_bits | jnp.left_shift(jnp.int32(1), 30 - i)
        cand = pltpu.bitcast(jnp.broadcast_to(cand_bits, (ne, SUBLANES, LANES)), F32)[:, :1, :1]
        return jnp.where(count(p >= cand) >= cap, cand_bits, prefix_bits)

    bits = lax.fori_loop(0, 31, step, jnp.zeros((ne, 1, 1), I32))
    thr = pltpu.bitcast(jnp.broadcast_to(bits, (ne, SUBLANES, LANES)), F32)[:, :1, :1]
    above = p > thr
    tie = p == thr
    need = cap - count(above)
    tw, to, _ = prefix(tie)
    sel = above | (tie & ((tw + to) < need))
    within, offs, tot = prefix(sel)
    rank = within + offs
    sel = sel & (rank < cap)
    gate_ref[...] = jnp.where(sel, p, 0.0)
    e_idx = lax.broadcasted_iota(I32, (ne, nr, LANES), 0)
    row_ref[...] = e_idx * row_base[0] + row_base[1] + jnp.minimum(rank, cap - 1).astype(I32)
    lane = lax.broadcasted_iota(I32, (ne, nr, LANES), 2)
    invalid = jnp.int32(1 << 20)
    code = jnp.where(sel, ((lane - within.astype(I32)) << 7) | lane, invalid)
    code = code.reshape(rows, LANES)
    for kbit in range(7):
        moved = pltpu.roll(code, LANES - (1 << kbit), 1)
        take = (moved < invalid) & (((moved >> (7 + kbit)) & 1) == 1)
        stay = (code < invalid) & (((code >> (7 + kbit)) & 1) == 0)
        code = jnp.where(take, moved, jnp.where(stay, code, invalid))
    lidx_ref[...] = jnp.where(code < invalid, code & (LANES - 1), 0).reshape(ne, nr, LANES)
    cnt_ref[...] = jnp.minimum(tot, jnp.maximum(cap - offs, 0.0)).astype(I32)


def _route(p3, cap, row_stride, row_off):
    ne, nr, _ = p3.shape
    full = lambda: pl.BlockSpec((ne, nr, LANES), lambda: (0, 0, 0))
    shp = lambda dt: jax.ShapeDtypeStruct((ne, nr, LANES), dt)
    return pl.pallas_call(
        functools.partial(_route_kernel, cap=cap, row_base=(row_stride, row_off)),
        in_specs=[full()],
        out_specs=[full(), full(), full(), full()],
        out_shape=[shp(F32), shp(I32), shp(I32), shp(I32)],
        compiler_params=pltpu.CompilerParams(vmem_limit_bytes=VMEM_LIMIT),
        name="route",
    )(p3)


def _dispatch_kernel(meta_ref, h_ref, xs_ref, stage_ref, fill_ref, sem_ref, *, rows_per_expert, nsteps):
    j = pl.program_id(0)
    ring = 2 * HALF

    @pl.when(j == 0)
    def _():
        for e in range(N_EXPERTS):
            fill_ref[e] = 0

    def flush(e, half, dst):
        return pltpu.make_async_copy(stage_ref.at[pl.ds(e * ring + half * HALF, HALF)],
                                     xs_ref.at[pl.ds(dst, HALF)], sem_ref.at[2 * e + half])

    cnt_row = N_EXPERTS * DSP_TILES
    for ti in range(DSP_TILES):
        for e in range(N_EXPERTS):
            seg = e * DSP_TILES + ti
            c = meta_ref[0, cnt_row, seg]
            f = fill_ref[e]
            pos = f & (HALF - 1)
            half = (f >> 7) & 1
            cross = pos + c >= HALF

            @pl.when(cross & (f >= HALF))
            def _():
                flush(e, 1 - half, 0).wait()

            def chunk(ch, _):
                for u in range(SUBLANES):
                    kk = jnp.minimum(ch * SUBLANES + u, c - 1)
                    li = meta_ref[0, seg, kk]
                    stage_ref[e * ring + ((f + kk) & (ring - 1))] = h_ref[ti * TILE + li]
                return 0

            lax.fori_loop(0, (c + SUBLANES - 1) >> 3, chunk, 0)

            @pl.when(cross)
            def _():
                flush(e, half, e * rows_per_expert + f - pos).start()

            fill_ref[e] = f + c

    @pl.when(j == nsteps - 1)
    def _():
        for e in range(N_EXPERTS):
            f = fill_ref[e]

            @pl.when(f >= HALF)
            def _():
                flush(e, ((f >> 7) - 1) & 1, 0).wait()


def _dispatch(meta, h3, rows_per_expert):
    n = h3.shape[0]
    step = DSP_TILES * TILE
    nsteps = n // step
    return pl.pallas_call(
        functools.partial(_dispatch_kernel, rows_per_expert=rows_per_expert, nsteps=nsteps),
        grid=(nsteps,),
        in_specs=[pl.BlockSpec((1, meta.shape[1], LANES), lambda j: (j, 0, 0), memory_space=pltpu.SMEM),
                  pl.BlockSpec((step, SUBLANES, LANES), lambda j: (j, 0, 0))],
        out_specs=pl.BlockSpec(memory_space=pl.ANY),
        out_shape=jax.ShapeDtypeStruct((N_EXPERTS * rows_per_expert, SUBLANES, LANES), F32),
        scratch_shapes=[pltpu.VMEM((N_EXPERTS * 2 * HALF, SUBLANES, LANES), F32),
                        pltpu.SMEM((N_EXPERTS,), I32), pltpu.SemaphoreType.DMA((2 * N_EXPERTS,))],
        compiler_params=_cparams(("arbitrary",)),
        name="dispatch",
    )(meta, h3)


def _expert_ffn_kernel(xs_ref, wg_ref, wu_ref, wd_ref, o_ref, xb_ref, acc_ref):
    f = pl.program_id(2)
    rt = xb_ref.shape[0]

    @pl.when(f == 0)
    def _():
        for s in range(ROW_CHUNKS):
            xb_ref[:, s * LANES:(s + 1) * LANES] = xs_ref[pl.ds(s, rt, stride=ROW_CHUNKS), :].astype(BF16)
        acc_ref[...] = jnp.zeros_like(acc_ref)

    xb = xb_ref[...]
    g = jnp.dot(xb, wg_ref[0], preferred_element_type=F32)
    u = jnp.dot(xb, wu_ref[0], preferred_element_type=F32)
    hdn = (g * jax.nn.sigmoid(g)) * u
    acc_ref[...] += jnp.dot(hdn.astype(BF16), wd_ref[0], preferred_element_type=F32)

    @pl.when(f == pl.num_programs(2) - 1)
    def _():
        o_ref[...] = acc_ref[...].astype(BF16)


def _expert_ffn(xs2, wg, wu, wd, rows_per_expert):
    rt = min(FFN_ROWS, rows_per_expert)
    nrt = rows_per_expert // rt
    nf = EXPERT_FF // FFN_FF
    return pl.pallas_call(
        _expert_ffn_kernel,
        grid=(N_EXPERTS, nrt, nf),
        in_specs=[pl.BlockSpec((rt * ROW_CHUNKS, LANES), lambda e, r, f: (e * nrt + r, 0)),
                  pl.BlockSpec((1, D_MODEL, FFN_FF), lambda e, r, f: (e, 0, f)),
                  pl.BlockSpec((1, D_MODEL, FFN_FF), lambda e, r, f: (e, 0, f)),
                  pl.BlockSpec((1, FFN_FF, D_MODEL), lambda e, r, f: (e, f, 0))],
        out_specs=pl.BlockSpec((rt, D_MODEL), lambda e, r, f: (e * nrt + r, 0)),
        out_shape=jax.ShapeDtypeStruct((N_EXPERTS * rows_per_expert, D_MODEL), BF16),
        scratch_shapes=[pltpu.VMEM((rt, D_MODEL), BF16), pltpu.VMEM((rt, D_MODEL), F32)],
        compiler_params=_cparams(("parallel", "parallel", "arbitrary")),
        name="expert_ffn",
    )(xs2, wg, wu, wd)


def _combine_kernel(start_ref, npass_ref, x_ref, row_ref, gate_ref, gf_ref, eo_ref, y_ref, slab_ref, sem_ref,
                    *, nblk, total_rows, final):
    j = pl.program_id(0)
    cs = CMB_SLAB
    max_base = total_rows - cs

    def base_of(blk, e, p):
        return jnp.minimum(start_ref[blk * N_EXPERTS + e] + p * cs, max_base)

    def copies(blk, p, slot):
        return [pltpu.make_async_copy(eo_ref.at[pl.ds(pl.multiple_of(base_of(blk, e, p), 16), cs)],
                                      slab_ref.at[slot, pl.ds(e * cs, cs)], sem_ref.at[slot])
                for e in range(N_EXPERTS)]

    slot = j & 1

    @pl.when(j == 0)
    def _():
        for cp in copies(0, 0, 0):
            cp.start()

    @pl.when(j + 1 < nblk)
    def _():
        for cp in copies(j + 1, 0, 1 - slot):
            cp.start()

    for cp in copies(j, 0, slot):
        cp.wait()

    rows = row_ref[...]
    gates = gate_ref[...]
    k_iota = lax.broadcasted_iota(I32, (cs, CMB_T), 0)

    def scatter(p, slot):
        parts = []
        for e in range(N_EXPERTS):
            lo = start_ref[j * N_EXPERTS + e] + p * cs
            r = rows[e:e + 1]
            hit = (k_iota == (r - base_of(j, e, p))) & (r >= lo) & (r < lo + cs)
            parts.append(jnp.where(hit, gates[e:e + 1], 0.0).astype(BF16))
        onehot = jnp.concatenate(parts, axis=0)
        return lax.dot_general(onehot, slab_ref[slot], (((0,), (0,)), ((), ())), preferred_element_type=F32)

    y_ref[...] = x_ref[...] + scatter(0, slot)

    def extra(p, _):
        spare = 1 - slot

        @pl.when(j + 1 < nblk)
        def _():
            for cp in copies(j + 1, 0, spare):
                cp.wait()

        for cp in copies(j, p, spare):
            cp.start()
        for cp in copies(j, p, spare):
            cp.wait()
        y_ref[...] += scatter(p, spare)

        @pl.when(j + 1 < nblk)
        def _():
            for cp in copies(j + 1, 0, spare):
                cp.start()
        return 0

    lax.fori_loop(1, npass_ref[j], extra, 0)
    if final:
        y_ref[...] = _rms(y_ref[...], gf_ref[...])


def _combine(starts, npass, x1, rowidx, gates, gf, eo, final):
    n = x1.shape[0]
    nblk = n // CMB_T
    total_rows = eo.shape[0]
    return pl.pallas_call(
        functools.partial(_combine_kernel, nblk=nblk, total_rows=total_rows, final=final),
        grid_spec=pltpu.PrefetchScalarGridSpec(
            num_scalar_prefetch=2,
            grid=(nblk,),
            in_specs=[pl.BlockSpec((CMB_T, D_MODEL), lambda j, *_: (j, 0)),
                      pl.BlockSpec((N_EXPERTS, CMB_T), lambda j, *_: (0, j)),
                      pl.BlockSpec((N_EXPERTS, CMB_T), lambda j, *_: (0, j)),
                      pl.BlockSpec((1, D_MODEL), lambda j, *_: (0, 0)),
                      pl.BlockSpec(memory_space=pl.ANY)],
            out_specs=pl.BlockSpec((CMB_T, D_MODEL), lambda j, *_: (j, 0)),
            scratch_shapes=[pltpu.VMEM((2, N_EXPERTS * CMB_SLAB, D_MODEL), BF16),
                            pltpu.SemaphoreType.DMA((2,))]),
        out_shape=jax.ShapeDtypeStruct((n, D_MODEL), F32),
        compiler_params=_cparams(("arbitrary",)),
        name="combine",
    )(starts, npass, x1, rowidx, gates, gf, eo)


def _rope_tables(smax):
    half = ROT_DIM // 2
    inv_freq = ROPE_THETA ** (-jnp.arange(half, dtype=F32) * 2.0 / ROT_DIM)
    ang = jnp.arange(smax, dtype=F32)[:, None] * inv_freq[None, :]
    cos, sin = jnp.cos(ang), jnp.sin(ang)
    pad = HEAD_DIM - ROT_DIM
    one = jnp.ones((smax, pad), F32)
    zero = jnp.zeros((smax, pad), F32)
    zh = jnp.zeros((smax, half), F32)
    c = jnp.concatenate([cos, cos, one], axis=1)
    s1 = jnp.concatenate([-sin, zh, zero], axis=1)
    s2 = jnp.concatenate([zh, sin, zero], axis=1)
    return tuple(jnp.concatenate([t] * (LANES // HEAD_DIM), axis=1) for t in (c, s1, s2))


def _pair_weights(w_a, w_x):
    def bd(w):
        w = w.reshape(2, LRU_HEADS // 2, 2, LRU_BLOCK, LRU_BLOCK)
        z = jnp.zeros_like(w[:, :, 0])
        top = jnp.concatenate([w[:, :, 0], z], axis=-1)
        bot = jnp.concatenate([z, w[:, :, 1]], axis=-1)
        return jnp.concatenate([top, bot], axis=-2)
    return jnp.concatenate([bd(w_a), bd(w_x)], axis=-1).astype(BF16)


def _forward(groups, x, norm1, w_in, attn_sink, g_attn_out, g_lru_out, conv_w, conv_b, lru_w_a, lru_b_a,
             lru_w_x, lru_b_x, lru_lambda, w_out, norm2, w_router, w_gate, w_up, w_down, final_norm):
    (b0, s0), (b1, s1) = groups
    n0, n1 = b0 * s0, b1 * s1
    n = n0 + n1
    depth = norm1.shape[0]
    for s in (s0, s1):
        assert s % max(TM_IN, TQ, TM_REC) == 0
    caps = [max(1, CAPACITY_FACTOR * m // N_EXPERTS) for m in (n0, n1)]
    for m, c in zip((n0, n1), caps):
        assert m % (DSP_TILES * TILE) == 0 and m % TM_OUT == 0 and c % HALF == 0
    rows_per_expert = caps[0] + caps[1]
    assert rows_per_expert % min(FFN_ROWS, rows_per_expert) == 0
    total_rows = N_EXPERTS * rows_per_expert
    rope = _rope_tables(max(s0, s1))
    nsteps = n // (DSP_TILES * TILE)
    for l in range(depth):
        q, k, v, xr, gt = _mix_in(x, norm1[l][None], w_in[l].astype(BF16), rope, n0, s0, s1)
        attn = _attention(q, k, v, attn_sink[l], n0, s0, s1)
        h = _recurrent(xr, conv_w[l], conv_b[l][None], _pair_weights(lru_w_a[l], lru_w_x[l]),
                       lru_b_a[l][:, None], lru_b_x[l][:, None], lru_lambda[l][:, None], n0, s0, s1)
        x1, h2, probs = _mix_out(x, attn, h, gt, g_attn_out[l][None], g_lru_out[l][None],
                                 w_out[l].astype(BF16), norm2[l][None], w_router[l].T)
        routed = []
        for lo, m, cap, off in ((0, n0, caps[0], 0), (n0, n1, caps[1], caps[0])):
            p3 = probs[:, lo:lo + m].reshape(N_EXPERTS, m // TILE, TILE)
            routed.append(_route(p3, cap, rows_per_expert, off))
        gate3, row3, lidx3, cnt3 = (jnp.concatenate([a, b], axis=1) for a, b in zip(*routed))
        cnt = cnt3[:, :, 0]
        lidx_steps = lidx3.reshape(N_EXPERTS, nsteps, DSP_TILES, TILE).transpose(1, 0, 2, 3)
        lidx_steps = lidx_steps.reshape(nsteps, N_EXPERTS * DSP_TILES, TILE)
        cnt_steps = cnt.reshape(N_EXPERTS, nsteps, DSP_TILES).transpose(1, 0, 2).reshape(nsteps, 1, -1)
        cnt_rows = jnp.pad(cnt_steps, ((0, 0), (0, SUBLANES - 1), (0, TILE - N_EXPERTS * DSP_TILES)))
        meta = jnp.concatenate([lidx_steps, cnt_rows], axis=1)
        xs = _dispatch(meta, h2.reshape(n, SUBLANES, LANES), rows_per_expert)
        eo = _expert_ffn(xs.reshape(total_rows * ROW_CHUNKS, LANES), w_gate[l].astype(BF16),
                         w_up[l].astype(BF16), w_down[l].astype(BF16), rows_per_expert)
        rowidx = row3.reshape(N_EXPERTS, n)
        gates = gate3.reshape(N_EXPERTS, n)
        per_blk = CMB_T // TILE
        first_row = rowidx[:, ::CMB_T]
        cnt_blk = cnt.reshape(N_EXPERTS, n // CMB_T, per_blk).sum(axis=2)
        starts = (first_row // 16) * 16
        span = (first_row - starts) + cnt_blk
        npass = jnp.maximum(1, jnp.max((span + CMB_SLAB - 1) // CMB_SLAB, axis=0)).astype(I32)
        x = _combine(starts.T.reshape(-1).astype(I32), npass, x1, rowidx, gates, final_norm[None], eo,
                     final=(l == depth - 1))
    return x


def kernel(x_prompt, x_sample, norm1, w_in, attn_sink, g_attn_out, g_lru_out, conv_w, conv_b, lru_w_a, lru_b_a,
           lru_w_x, lru_b_x, lru_lambda, w_out, norm2, w_router, w_gate, w_up, w_down, final_norm):
    groups = (x_prompt.shape[:2], x_sample.shape[:2])
    x = jnp.concatenate([x_prompt.reshape(-1, D_MODEL), x_sample.reshape(-1, D_MODEL)], axis=0)
    y = _forward(groups, x, norm1, w_in, attn_sink, g_attn_out, g_lru_out, conv_w, conv_b, lru_w_a, lru_b_a,
                 lru_w_x, lru_b_x, lru_lambda, w_out, norm2, w_router, w_gate, w_up, w_down, final_norm)
    n0 = x_prompt.shape[0] * x_prompt.shape[1]
    return y[:n0].reshape(x_prompt.shape), y[n0:].reshape(x_sample.shape)
```
